```python
import math
import jax, jax.numpy as jnp
from jax import lax
import numpy as np

D_MODEL = 1024
BATCH = 4
SEQ = 8192
DEPTH = 1
DEC_BATCH = 32
DEC_SEQ = 64
PAST_LEN = 1024

CHUNK = 64
RET_HEADS = 8
RET_DK = 64
RET_DV = 64
RET_QK = RET_HEADS * RET_DK
RET_WIDTH = RET_HEADS * RET_DV
RET_THETA = 10000.0
DIFF_HEADS = 4
DIFF_DK = 64
DIFF_DV = 2 * DIFF_DK
DIFF_QK = DIFF_HEADS * 2 * DIFF_DK
DIFF_WIDTH = DIFF_HEADS * DIFF_DV
ROPE_THETA = 500000.0
ROPE_DIM = DIFF_DK // 4
MIX_WIDTH = RET_WIDTH + DIFF_WIDTH
Q_BLOCK = 128
EPS = 1e-6
NEG_BIG = -1e30
IN_SIZES = (RET_QK, RET_QK, RET_WIDTH, RET_WIDTH, DIFF_QK, DIFF_QK, DIFF_WIDTH, DIFF_WIDTH)
IN_SPLITS = tuple(sum(IN_SIZES[:i + 1]) for i in range(len(IN_SIZES) - 1))
IN_WIDTH = sum(IN_SIZES)

kernel_name = 'hybrid_retention_diffattn_stream_step'


def rmsnorm(x, g):
    xf = x.astype(jnp.float32)
    y = xf * lax.rsqrt(jnp.mean(xf * xf, axis=-1, keepdims=True) + EPS)
    return (y * g.astype(jnp.float32)).astype(x.dtype)


def rotary_tables(pos, dim, theta):
    inv_freq = 1.0 / (theta ** (jnp.arange(0, dim, 2, dtype=jnp.float32) / dim))
    ang = pos[:, None] * inv_freq[None, :]
    return jnp.cos(ang), jnp.sin(ang)


def rotate_half(x, cos, sin):
    half = x.shape[-1] // 2
    x1, x2 = x[..., :half], x[..., half:]
    cos = cos.astype(x.dtype)
    sin = sin.astype(x.dtype)
    return jnp.concatenate([x1 * cos - x2 * sin, x2 * cos + x1 * sin], axis=-1)


def retention_log_decay():
    return jnp.log1p(-jnp.exp2(-5.0 - jnp.arange(RET_HEADS, dtype=jnp.float32)))


def branch_inputs(x, norm_g, w_in, pos):
    b, l, _ = x.shape
    h = rmsnorm(x, norm_g)
    z = jnp.einsum('bld,de->ble', h, w_in)
    rq, rk, rv, rg, dq, dk, dv, dg = jnp.split(z, IN_SPLITS, axis=-1)
    rcos, rsin = rotary_tables(pos, RET_DK, RET_THETA)
    rcos, rsin = rcos[None, :, None, :], rsin[None, :, None, :]
    rq = rotate_half(rq.reshape(b, l, RET_HEADS, RET_DK), rcos, rsin)
    rk = rotate_half(rk.reshape(b, l, RET_HEADS, RET_DK), rcos, rsin) * (RET_DK ** -0.5)
    rv = rv.reshape(b, l, RET_HEADS, RET_DV)
    dcos, dsin = rotary_tables(pos, ROPE_DIM, ROPE_THETA)
    dcos, dsin = dcos[None, :, None, None, :], dsin[None, :, None, None, :]

    def partial_rope(t):
        t = t.reshape(b, l, DIFF_HEADS, 2, DIFF_DK)
        return jnp.concatenate([rotate_half(t[..., :ROPE_DIM], dcos, dsin), t[..., ROPE_DIM:]], axis=-1)

    dq = partial_rope(dq)
    dk = partial_rope(dk)
    dv = dv.reshape(b, l, DIFF_HEADS, DIFF_DV)
    return rq, rk, rv, rg, dq, dk, dv, dg


def retention_block(q, k, v, s, log_g):
    q = q.astype(jnp.float32)
    k = k.astype(jnp.float32)
    v = v.astype(jnp.float32)
    s = s.astype(jnp.float32)
    l = q.shape[1]
    idx = jnp.arange(l, dtype=jnp.float32)
    rel = idx[:, None] - idx[None, :]
    decay = jnp.where((rel >= 0)[None], jnp.exp(log_g[:, None, None] * jnp.maximum(rel, 0.0)[None]), 0.0)
    scores = jnp.einsum('bqhd,bkhd->bhqk', q, k) * decay[None]
    inner = jnp.einsum('bhqk,bkhe->bqhe', scores, v)
    q_decay = jnp.exp(log_g[None, :] * (idx[:, None] + 1.0))
    cross = jnp.einsum('bqhd,bhde->bqhe', q, s) * q_decay[None, :, :, None]
    k_decay = jnp.exp(log_g[None, :] * (l - 1.0 - idx[:, None]))
    s_new = (jnp.exp(log_g * l)[None, :, None, None] * s
             + jnp.einsum('bkhd,bkhe->bhde', k * k_decay[None, :, :, None], v))
    return inner + cross, s_new


def retention_prompt(q, k, v, log_g):
    b, l = q.shape[0], q.shape[1]
    nc = l // CHUNK

    def to_blocks(t):
        return t.reshape(b, nc, CHUNK, RET_HEADS, t.shape[-1]).swapaxes(0, 1)

    s0 = jnp.zeros((b, RET_HEADS, RET_DK, RET_DV), jnp.float32)

    def step(s, blk):
        qc, kc, vc = blk
        o, s = retention_block(qc, kc, vc, s, log_g)
        return s, o

    s, o = lax.scan(step, s0, (to_blocks(q), to_blocks(k), to_blocks(v)))
    return o.swapaxes(0, 1).reshape(b, l, RET_HEADS, RET_DV), s


def diff_attend(q, k, v, lam, mask):
    logits = jnp.einsum('bqhcd,bkhcd->bhcqk', q, k, preferred_element_type=jnp.float32) * (DIFF_DK ** -0.5)
    if mask is not None:
        logits = jnp.where(mask, logits, NEG_BIG)
    p = jax.nn.softmax(logits, axis=-1)
    w = p[:, :, 0] - lam * p[:, :, 1]
    o = jnp.einsum('bhqk,bkhe->bqhe', w, v.astype(jnp.float32))
    return o.astype(v.dtype)


def diff_attention_prompt(q, k, v, lam):
    b, l = q.shape[0], q.shape[1]
    nb = l // Q_BLOCK
    q_blocks = q.reshape(b, nb, Q_BLOCK, DIFF_HEADS, 2, DIFF_DK).swapaxes(0, 1)
    key_chunk = jnp.arange(l) // CHUNK

    def one_block(args):
        q_blk, blk = args
        q_chunk = (blk * Q_BLOCK + jnp.arange(Q_BLOCK)) // CHUNK
        mask = key_chunk[None, :] <= q_chunk[:, None]
        return diff_attend(q_blk, k, v, lam, mask)

    o = lax.map(one_block, (q_blocks, jnp.arange(nb)))
    return o.swapaxes(0, 1).reshape(b, l, DIFF_HEADS, DIFF_DV)


def merge_branches(ret_o, ret_g, diff_o, diff_g, ret_norm_g, diff_norm_g, lambda_init, w_out):
    b, l = ret_o.shape[0], ret_o.shape[1]
    ret = rmsnorm(ret_o, ret_norm_g.reshape(RET_HEADS, RET_DV)).reshape(b, l, RET_WIDTH)
    dif = (rmsnorm(diff_o, diff_norm_g) * (1.0 - lambda_init)).reshape(b, l, DIFF_WIDTH)
    mixed = jnp.concatenate([jax.nn.silu(ret_g) * ret, jax.nn.silu(diff_g) * dif], axis=-1)
    return jnp.einsum('ble,ed->bld', mixed, w_out)


def setup_inputs(seed: int = 0) -> dict:
    key = jax.random.key(seed)
    ks = jax.random.split(key, 15)
    f32 = jnp.float32
    nrm = jax.random.normal
    return {
        'x_prompt': nrm(ks[0], (BATCH, SEQ, D_MODEL), f32),
        'x_sample': nrm(ks[1], (DEC_BATCH, DEC_SEQ, D_MODEL), f32),
        'cache_k': nrm(ks[2], (DEPTH, DEC_BATCH, PAST_LEN, DIFF_HEADS, DIFF_DV), f32),
        'cache_v': nrm(ks[3], (DEPTH, DEC_BATCH, PAST_LEN, DIFF_HEADS, DIFF_DV), f32),
        'state_ret': nrm(ks[4], (DEPTH, DEC_BATCH, RET_HEADS, RET_DK, RET_DV), f32),
        'norm_g': 1.0 + 0.02 * nrm(ks[5], (DEPTH, D_MODEL), f32),
        'w_in': nrm(ks[6], (DEPTH, D_MODEL, IN_WIDTH), f32) * (D_MODEL ** -0.5),
        'w_out': nrm(ks[7], (DEPTH, MIX_WIDTH, D_MODEL), f32) * (MIX_WIDTH ** -0.5),
        'ret_norm_g': 1.0 + 0.02 * nrm(ks[8], (DEPTH, RET_WIDTH), f32),
        'diff_norm_g': 1.0 + 0.02 * nrm(ks[9], (DEPTH, DIFF_DV), f32),
        'lam_q1': 0.1 * nrm(ks[10], (DEPTH, DIFF_DK), f32),
        'lam_k1': 0.1 * nrm(ks[11], (DEPTH, DIFF_DK), f32),
        'lam_q2': 0.1 * nrm(ks[12], (DEPTH, DIFF_DK), f32),
        'lam_k2': 0.1 * nrm(ks[13], (DEPTH, DIFF_DK), f32),
        'final_norm_g': 1.0 + 0.02 * nrm(ks[14], (D_MODEL,), f32),
    }


def reference(x_prompt, x_sample, cache_k, cache_v, state_ret, norm_g, w_in, w_out, ret_norm_g,
              diff_norm_g, lam_q1, lam_k1, lam_q2, lam_k2, final_norm_g):
    past = cache_k.shape[2]
    pos_p = jnp.arange(x_prompt.shape[1], dtype=jnp.float32)
    pos_s = past + jnp.arange(x_sample.shape[1], dtype=jnp.float32)
    log_g = retention_log_decay()
    xp, xs = x_prompt, x_sample
    bp, lp = xp.shape[0], xp.shape[1]
    bs = xs.shape[0]
    ret_p, ret_s, kp, vp, ks_new, vs_new = [], [], [], [], [], []
    for layer in range(DEPTH):
        lambda_init = 0.8 - 0.6 * math.exp(-0.3 * layer)
        lam = (jnp.exp(jnp.sum(lam_q1[layer].astype(jnp.float32) * lam_k1[layer].astype(jnp.float32)))
               - jnp.exp(jnp.sum(lam_q2[layer].astype(jnp.float32) * lam_k2[layer].astype(jnp.float32)))
               + lambda_init)
        rq, rk, rv, rg, dq, dk, dv, dg = branch_inputs(xp, norm_g[layer], w_in[layer], pos_p)
        ro, rs = retention_prompt(rq, rk, rv, log_g)
        do = diff_attention_prompt(dq, dk, dv, lam)
        xp = xp + merge_branches(ro.astype(xp.dtype), rg, do, dg, ret_norm_g[layer], diff_norm_g[layer],
                                 lambda_init, w_out[layer])
        ret_p.append(rs)
        kp.append(dk.reshape(bp, lp, DIFF_HEADS, DIFF_DV))
        vp.append(dv)
        srq, srk, srv, srg, sdq, sdk, sdv, sdg = branch_inputs(xs, norm_g[layer], w_in[layer], pos_s)
        sro, sst = retention_block(srq, srk, srv, state_ret[layer], log_g)
        k_all = jnp.concatenate(
            [cache_k[layer].reshape(bs, past, DIFF_HEADS, 2, DIFF_DK).astype(sdk.dtype), sdk], axis=1)
        v_all = jnp.concatenate([cache_v[layer].astype(sdv.dtype), sdv], axis=1)
        sdo = diff_attend(sdq, k_all, v_all, lam, None)
        xs = xs + merge_branches(sro.astype(xs.dtype), srg, sdo, sdg, ret_norm_g[layer], diff_norm_g[layer],
                                 lambda_init, w_out[layer])
        ret_s.append(sst)
        ks_new.append(sdk.reshape(bs, xs.shape[1], DIFF_HEADS, DIFF_DV))
        vs_new.append(sdv)
    y_prompt = rmsnorm(xp, final_norm_g)
    y_sample = rmsnorm(xs, final_norm_g)
    return (y_prompt, y_sample, jnp.stack(ret_p), jnp.stack(ret_s), jnp.stack(kp), jnp.stack(vp),
            jnp.stack(ks_new), jnp.stack(vs_new))
```

```python
import functools
import math

import jax
import jax.numpy as jnp
from jax import lax
from jax.experimental import pallas as pl
from jax.experimental.pallas import tpu as pltpu

F32 = jnp.float32
BF16 = jnp.bfloat16

LANES = 128
VMEM_LIMIT_BYTES = 56 * 1024 * 1024

CHUNK = 64
RET_HEADS = 8
RET_DK = 64
RET_DV = 64
RET_WIDTH = RET_HEADS * RET_DV
RET_THETA = 10000.0
DIFF_HEADS = 4
DIFF_DK = 64
DIFF_DV = 2 * DIFF_DK
DIFF_WIDTH = DIFF_HEADS * DIFF_DV
ROPE_THETA = 500000.0
ROPE_DIM = DIFF_DK // 4
EPS = 1e-6
SEG = 512
N_SEG = 8
HEAD_PAIRS = RET_WIDTH // LANES

PROJ_TILE = 512
RET_CHUNK_PROMPT = 256
ATT_TQ = 512
ATT_TK = 512


def _dot(a, b):
    return jnp.dot(a, b, preferred_element_type=F32)


def _dot_nt(a, b):
    return lax.dot_general(a, b, (((1,), (1,)), ((), ())), preferred_element_type=F32)


def _lane_iota(shape):
    return lax.broadcasted_iota(jnp.int32, shape, len(shape) - 1)


def _rope_block(xb, cos, sin, first_half, shift_up, shift_down):
    partner = jnp.where(first_half, pltpu.roll(xb, shift_up, 1), pltpu.roll(xb, shift_down, 1))
    return xb * cos + partner * sin


def _project_kernel(x_ref, g_ref, w_ref, rcos_ref, rsin_ref, dcos_ref, dsin_ref,
                    rq_ref, rk_ref, rv_ref, rg_ref, dq_ref, dkb_ref, dvb_ref, dg_ref,
                    dk_ref, dv_ref):
    x = x_ref[...]
    ms = jnp.mean(x * x, axis=-1, keepdims=True)
    h = (x * lax.rsqrt(ms + EPS)) * g_ref[...]
    hb = h.astype(BF16)

    lane = _lane_iota((x.shape[0], LANES))
    ret_first = (lane % RET_DK) < (RET_DK // 2)
    diff_first = (lane % DIFF_DK) < (ROPE_DIM // 2)
    rcos, rsin = rcos_ref[...], rsin_ref[...]
    dcos, dsin = dcos_ref[...], dsin_ref[...]

    def seg(i):
        return _dot(hb, w_ref[:, i * SEG:(i + 1) * SEG])

    def ret_rope(z, scale, out_ref):
        for c in range(SEG // LANES):
            sl = slice(c * LANES, (c + 1) * LANES)
            r = _rope_block(z[:, sl], rcos, rsin, ret_first, LANES - RET_DK // 2, RET_DK // 2)
            out_ref[:, sl] = (r * scale).astype(out_ref.dtype)

    def diff_rope(z, scale, out_refs):
        for c in range(SEG // LANES):
            sl = slice(c * LANES, (c + 1) * LANES)
            r = _rope_block(z[:, sl], dcos, dsin, diff_first, LANES - ROPE_DIM // 2, ROPE_DIM // 2)
            r = r * scale
            for o in out_refs:
                o[:, sl] = r.astype(o.dtype)

    ret_rope(seg(0), 1.0, rq_ref)
    ret_rope(seg(1), RET_DK ** -0.5, rk_ref)
    rv_ref[...] = seg(2).astype(BF16)
    rg_ref[...] = seg(3)
    diff_rope(seg(4), DIFF_DK ** -0.5, (dq_ref,))
    diff_rope(seg(5), 1.0, (dk_ref, dkb_ref))
    zv = seg(6)
    dv_ref[...] = zv
    dvb_ref[...] = zv.astype(BF16)
    dg_ref[...] = seg(7)


def _project(x2d, norm_g, w_bf16, tabs, n_seq_tiles, n_rep):
    n, d = x2d.shape
    tile = PROJ_TILE
    assert n == n_seq_tiles * n_rep * tile
    row = lambda j, r: (r * n_seq_tiles + j, 0)
    tab = lambda j, r: (j, 0)
    const = lambda j, r: (0, 0)
    out_block = pl.BlockSpec((tile, SEG), row)
    bf = jax.ShapeDtypeStruct((n, SEG), BF16)
    f32 = jax.ShapeDtypeStruct((n, SEG), F32)
    return pl.pallas_call(
        _project_kernel,
        grid=(n_seq_tiles, n_rep),
        in_specs=[
            pl.BlockSpec((tile, d), row),
            pl.BlockSpec((1, d), const),
            pl.BlockSpec((d, N_SEG * SEG), const, pipeline_mode=pl.Buffered(1)),
            pl.BlockSpec((tile, LANES), tab),
            pl.BlockSpec((tile, LANES), tab),
            pl.BlockSpec((tile, LANES), tab),
            pl.BlockSpec((tile, LANES), tab),
        ],
        out_specs=[out_block] * 10,
        out_shape=[bf, bf, bf, f32, bf, bf, bf, f32, f32, f32],
        compiler_params=pltpu.CompilerParams(
            dimension_semantics=("arbitrary", "arbitrary"), vmem_limit_bytes=VMEM_LIMIT_BYTES),
        name="project",
    )(x2d, norm_g, w_bf16, *tabs)


def _retain_kernel(q_ref, k_ref, v_ref, s0_ref, dec_ref, qdec_ref, kdec_ref, gc_ref,
                   o_ref, sout_ref, s_scr):
    j = pl.program_id(1)

    @pl.when(j == 0)
    def _():
        s_scr[...] = s0_ref[...]

    c = q_ref.shape[0]
    lane = _lane_iota((c, LANES))
    low = lane < RET_DK
    row_i = lax.broadcasted_iota(jnp.int32, (LANES, LANES), 0)
    col_i = lax.broadcasted_iota(jnp.int32, (LANES, LANES), 1)
    same_head = (row_i < RET_DK) == (col_i < RET_DK)

    for p in range(HEAD_PAIRS):
        sl = slice(p * LANES, (p + 1) * LANES)
        q2, k2, v2 = q_ref[:, sl], k_ref[:, sl], v_ref[:, sl]
        s = s_scr[p]
        cross = _dot(q2, s.astype(BF16)) * qdec_ref[:, sl]
        zero = jnp.zeros_like(q2)
        sa = _dot_nt(jnp.where(low, q2, zero), k2) * dec_ref[2 * p]
        sb = _dot_nt(jnp.where(low, zero, q2), k2) * dec_ref[2 * p + 1]
        inner = jnp.where(low, _dot(sa.astype(BF16), v2), _dot(sb.astype(BF16), v2))
        o_ref[:, sl] = inner + cross
        kd = k2.astype(F32) * kdec_ref[:, sl]
        upd = _dot(kd.T.astype(BF16), v2)
        s_scr[p] = gc_ref[:, sl] * s + jnp.where(same_head, upd, 0.0)

    @pl.when(j == pl.num_programs(1) - 1)
    def _():
        sout_ref[...] = s_scr[...]


def _retention_tables(log_g, c):
    idx = jnp.arange(c, dtype=F32)
    rel = idx[:, None] - idx[None, :]
    dec = jnp.where((rel >= 0)[None], jnp.exp(log_g[:, None, None] * jnp.maximum(rel, 0.0)[None]), 0.0)
    per_lane = lambda t: jnp.repeat(t, RET_DV, axis=-1)
    qdec = per_lane(jnp.exp(log_g[None, :] * (idx[:, None] + 1.0)))
    kdec = per_lane(jnp.exp(log_g[None, :] * (c - 1.0 - idx[:, None])))
    gc = per_lane(jnp.exp(log_g * c)[None, :])
    return dec, qdec, kdec, gc


def _retain(q, k, v, s0_bd, log_g, batch, seq, c):
    n_chunks = seq // c
    dec, qdec, kdec, gc = _retention_tables(log_g, c)
    row = lambda b, j: (b * n_chunks + j, 0)
    blk = pl.BlockSpec((c, RET_WIDTH), row)
    st = pl.BlockSpec((None, HEAD_PAIRS, LANES, LANES), lambda b, j: (b, 0, 0, 0))
    return pl.pallas_call(
        _retain_kernel,
        grid=(batch, n_chunks),
        in_specs=[
            blk, blk, blk, st,
            pl.BlockSpec((RET_HEADS, c, c), lambda b, j: (0, 0, 0)),
            pl.BlockSpec((c, RET_WIDTH), lambda b, j: (0, 0)),
            pl.BlockSpec((c, RET_WIDTH), lambda b, j: (0, 0)),
            pl.BlockSpec((1, RET_WIDTH), lambda b, j: (0, 0)),
        ],
        out_specs=[blk, st],
        out_shape=[jax.ShapeDtypeStruct((batch * seq, RET_WIDTH), F32),
                   jax.ShapeDtypeStruct((batch, HEAD_PAIRS, LANES, LANES), F32)],
        scratch_shapes=[pltpu.VMEM((HEAD_PAIRS, LANES, LANES), F32)],
        compiler_params=pltpu.CompilerParams(
            dimension_semantics=("arbitrary", "arbitrary"), vmem_limit_bytes=VMEM_LIMIT_BYTES),
        name="retain",
    )(q, k, v, s0_bd, dec, qdec, kdec, gc)


def _state_to_blockdiag(s):
    b = s.shape[0]
    s = s.reshape(b, HEAD_PAIRS, 2, RET_DK, RET_DV)
    z = jnp.zeros_like(s[:, :, 0])
    top = jnp.concatenate([s[:, :, 0], z], axis=-1)
    bot = jnp.concatenate([z, s[:, :, 1]], axis=-1)
    return jnp.concatenate([top, bot], axis=-2)


def _blockdiag_to_state(sb):
    b = sb.shape[0]
    a = sb[:, :, :RET_DK, :RET_DV]
    d = sb[:, :, RET_DK:, RET_DV:]
    return jnp.stack([a, d], axis=2).reshape(b, RET_HEADS, RET_DK, RET_DV)


def _lambda_value(lq1_ref, lk1_ref, lq2_ref, lk2_ref, lambda_init):
    a = jnp.sum(lq1_ref[...] * lk1_ref[...], axis=-1, keepdims=True)
    b = jnp.sum(lq2_ref[...] * lk2_ref[...], axis=-1, keepdims=True)
    return jnp.exp(a) - jnp.exp(b) + lambda_init


def _attend_prompt_kernel(lq1_ref, lk1_ref, lq2_ref, lk2_ref, q_ref, k_ref, v_ref, o_ref,
                          m_scr, l_scr, acc_scr, *, lambda_init):
    i = pl.program_id(2)
    tq, tk = ATT_TQ, ATT_TK
    q = q_ref[...]
    lane = _lane_iota(q.shape)
    zero = jnp.zeros_like(q)
    qs = (jnp.where(lane < DIFF_DK, q, zero), jnp.where(lane < DIFF_DK, zero, q))

    m_scr[...] = jnp.full(m_scr.shape, -jnp.inf, F32)
    l_scr[...] = jnp.zeros(l_scr.shape, F32)
    acc_scr[...] = jnp.zeros(acc_scr.shape, F32)

    def step(j, masked):
        k = k_ref[pl.ds(pl.multiple_of(j * tk, tk), tk), :]
        v = v_ref[pl.ds(pl.multiple_of(j * tk, tk), tk), :]
        if masked:
            qc = lax.broadcasted_iota(jnp.int32, (tq, tk), 0) // CHUNK
            kc = lax.broadcasted_iota(jnp.int32, (tq, tk), 1) // CHUNK
            visible = kc <= qc
        for c in range(2):
            s = _dot_nt(qs[c], k)
            if masked:
                s = jnp.where(visible, s, -jnp.inf)
            m_old = m_scr[c]
            m_new = jnp.maximum(m_old, jnp.max(s, axis=-1, keepdims=True))
            alpha = jnp.exp(m_old - m_new)
            p = jnp.exp(s - m_new)
            l_scr[c] = alpha * l_scr[c] + jnp.sum(p, axis=-1, keepdims=True)
            acc_scr[c] = alpha * acc_scr[c] + _dot(p.astype(BF16), v)
            m_scr[c] = m_new

    def full_step(j, carry):
        step(j, False)
        return carry

    lax.fori_loop(0, i, full_step, 0)
    step(i, True)

    lam = _lambda_value(lq1_ref, lk1_ref, lq2_ref, lk2_ref, lambda_init)
    o_ref[...] = acc_scr[0] / l_scr[0] - lam * (acc_scr[1] / l_scr[1])


def _attend_prompt(q, k, v, lams, lambda_init, batch, seq):
    assert ATT_TQ == ATT_TK and ATT_TQ % CHUNK == 0 and seq % ATT_TQ == 0
    nq = seq // ATT_TQ
    lam_spec = pl.BlockSpec((1, DIFF_DK), lambda b, h, i: (0, 0))
    kv_spec = pl.BlockSpec((seq, DIFF_DV), lambda b, h, i: (b, h))
    qo_spec = pl.BlockSpec((ATT_TQ, DIFF_DV), lambda b, h, i: (b * nq + i, h))
    return pl.pallas_call(
        functools.partial(_attend_prompt_kernel, lambda_init=lambda_init),
        grid=(batch, DIFF_HEADS, nq),
        in_specs=[lam_spec] * 4 + [qo_spec, kv_spec, kv_spec],
        out_specs=qo_spec,
        out_shape=jax.ShapeDtypeStruct((batch * seq, DIFF_WIDTH), F32),
        scratch_shapes=[pltpu.VMEM((2, ATT_TQ, 1), F32), pltpu.VMEM((2, ATT_TQ, 1), F32),
                        pltpu.VMEM((2, ATT_TQ, DIFF_DV), F32)],
        compiler_params=pltpu.CompilerParams(
            dimension_semantics=("arbitrary", "arbitrary", "arbitrary"),
            vmem_limit_bytes=VMEM_LIMIT_BYTES),
        name="attend_prompt",
    )(*lams, q, k, v)


def _attend_sample_kernel(lq1_ref, lk1_ref, lq2_ref, lk2_ref, q_ref, kn_ref, vn_ref, kc_ref, vc_ref,
                          o_ref, *, lambda_init):
    lam = _lambda_value(lq1_ref, lk1_ref, lq2_ref, lk2_ref, lambda_init)
    lane = _lane_iota((q_ref.shape[0], LANES))
    for h in range(DIFF_HEADS):
        sl = slice(h * DIFF_DV, (h + 1) * DIFF_DV)
        q, kn, vn = q_ref[:, sl], kn_ref[:, sl], vn_ref[:, sl]
        kc = kc_ref[:, sl].astype(BF16)
        vc = vc_ref[:, sl].astype(BF16)
        zero = jnp.zeros_like(q)
        w_c = None
        w_n = None
        for c in range(2):
            qc = jnp.where(lane < DIFF_DK, q, zero) if c == 0 else jnp.where(lane < DIFF_DK, zero, q)
            s_c = _dot_nt(qc, kc)
            s_n = _dot_nt(qc, kn)
            m = jnp.maximum(jnp.max(s_c, axis=-1, keepdims=True), jnp.max(s_n, axis=-1, keepdims=True))
            p_c = jnp.exp(s_c - m)
            p_n = jnp.exp(s_n - m)
            inv = 1.0 / (jnp.sum(p_c, axis=-1, keepdims=True) + jnp.sum(p_n, axis=-1, keepdims=True))
            if c == 0:
                w_c, w_n = p_c * inv, p_n * inv
            else:
                w_c, w_n = w_c - lam * (p_c * inv), w_n - lam * (p_n * inv)
        o_ref[:, sl] = _dot(w_c.astype(BF16), vc) + _dot(w_n.astype(BF16), vn)


def _attend_sample(q, kn, vn, cache_k, cache_v, lams, lambda_init, batch, seq):
    past = cache_k.shape[1]
    lam_spec = pl.BlockSpec((1, DIFF_DK), lambda b: (0, 0))
    tok = pl.BlockSpec((seq, DIFF_WIDTH), lambda b: (b, 0))
    cache = pl.BlockSpec((None, past, DIFF_WIDTH), lambda b: (b, 0, 0))
    return pl.pallas_call(
        functools.partial(_attend_sample_kernel, lambda_init=lambda_init),
        grid=(batch,),
        in_specs=[lam_spec] * 4 + [tok, tok, tok, cache, cache],
        out_specs=tok,
        out_shape=jax.ShapeDtypeStruct((batch * seq, DIFF_WIDTH), F32),
        compiler_params=pltpu.CompilerParams(
            dimension_semantics=("arbitrary",), vmem_limit_bytes=VMEM_LIMIT_BYTES),
        name="attend_sample",
    )(*lams, q, kn, vn, cache_k, cache_v)


def _silu(g):
    return g * (1.0 / (1.0 + jnp.exp(-g)))


def _merge_kernel(ret_ref, rg_ref, dif_ref, dg_ref, x_ref, w_ref, rng_ref, dng_ref, fg_ref, y_ref,
                  *, diff_scale):
    tile = x_ref.shape[0]
    lane = _lane_iota((tile, LANES))
    low = lane < RET_DV
    ret_parts, dif_parts = [], []
    for c in range(RET_WIDTH // LANES):
        sl = slice(c * LANES, (c + 1) * LANES)
        r = ret_ref[:, sl]
        sq = r * r
        ss_a = jnp.sum(jnp.where(low, sq, 0.0), axis=-1, keepdims=True)
        ss_b = jnp.sum(jnp.where(low, 0.0, sq), axis=-1, keepdims=True)
        ms = jnp.where(low, ss_a, ss_b) * (1.0 / RET_DV)
        normed = (r * lax.rsqrt(ms + EPS)) * rng_ref[:, sl]
        ret_parts.append((_silu(rg_ref[:, sl]) * normed).astype(BF16))
    for c in range(DIFF_HEADS):
        sl = slice(c * DIFF_DV, (c + 1) * DIFF_DV)
        d = dif_ref[:, sl]
        ms = jnp.mean(d * d, axis=-1, keepdims=True)
        normed = ((d * lax.rsqrt(ms + EPS)) * dng_ref[:, sl]) * diff_scale
        dif_parts.append((_silu(dg_ref[:, sl]) * normed).astype(BF16))
    mixed_ret = jnp.concatenate(ret_parts, axis=-1)
    mixed_dif = jnp.concatenate(dif_parts, axis=-1)
    out = _dot(mixed_ret, w_ref[:RET_WIDTH, :]) + _dot(mixed_dif, w_ref[RET_WIDTH:, :])
    xo = x_ref[...] + out
    ms = jnp.mean(xo * xo, axis=-1, keepdims=True)
    y_ref[...] = (xo * lax.rsqrt(ms + EPS)) * fg_ref[...]


def _merge(ret_o, rg, dif_o, dg, x2d, w_out_bf16, ret_norm_g, diff_norm_g, final_g, diff_scale):
    n, d = x2d.shape
    tile = PROJ_TILE
    row = lambda i: (i, 0)
    const = lambda i: (0, 0)
    half = pl.BlockSpec((tile, SEG), row)
    return pl.pallas_call(
        functools.partial(_merge_kernel, diff_scale=diff_scale),
        grid=(n // tile,),
        in_specs=[half, half, half, half,
                  pl.BlockSpec((tile, d), row),
                  pl.BlockSpec((RET_WIDTH + DIFF_WIDTH, d), const, pipeline_mode=pl.Buffered(1)),
                  pl.BlockSpec((1, RET_WIDTH), const),
                  pl.BlockSpec((1, DIFF_WIDTH), const),
                  pl.BlockSpec((1, d), const)],
        out_specs=pl.BlockSpec((tile, d), row),
        out_shape=jax.ShapeDtypeStruct((n, d), F32),
        compiler_params=pltpu.CompilerParams(
            dimension_semantics=("arbitrary",), vmem_limit_bytes=VMEM_LIMIT_BYTES),
        name="merge",
    )(ret_o, rg, dif_o, dg, x2d, w_out_bf16, ret_norm_g, diff_norm_g, final_g)


def _rope_tables(pos):
    def tables(dim, theta, block):
        inv_freq = 1.0 / (theta ** (jnp.arange(0, dim, 2, dtype=F32) / dim))
        ang = pos[:, None] * inv_freq[None, :]
        cos, sin = jnp.cos(ang), jnp.sin(ang)
        pad = block - dim
        ones = jnp.ones((pos.shape[0], pad), F32)
        zeros = jnp.zeros((pos.shape[0], pad), F32)
        cos_b = jnp.concatenate([cos, cos, ones], axis=-1)
        sin_b = jnp.concatenate([-sin, sin, zeros], axis=-1)
        reps = LANES // block
        return jnp.tile(cos_b, (1, reps)), jnp.tile(sin_b, (1, reps))

    rcos, rsin = tables(RET_DK, RET_THETA, RET_DK)
    dcos, dsin = tables(ROPE_DIM, ROPE_THETA, DIFF_DK)
    return rcos, rsin, dcos, dsin


def kernel(x_prompt, x_sample, cache_k, cache_v, state_ret, norm_g, w_in, w_out, ret_norm_g, diff_norm_g,
           lam_q1, lam_k1, lam_q2, lam_k2, final_norm_g):
    depth = w_in.shape[0]
    assert depth == 1, "single-layer trunk"
    bp, lp, d = x_prompt.shape
    bs, ls, _ = x_sample.shape
    past = cache_k.shape[2]
    lambda_init = 0.8 - 0.6 * math.exp(-0.3 * 0)
    log_g = jnp.log1p(-jnp.exp2(-5.0 - jnp.arange(RET_HEADS, dtype=F32)))

    w_in_b = w_in[0].astype(BF16)
    w_out_b = w_out[0].astype(BF16)
    g_in = norm_g[0][None, :]
    rng = ret_norm_g[0][None, :]
    dng = jnp.tile(diff_norm_g[0], DIFF_HEADS)[None, :]
    fg = final_norm_g[None, :]
    lams = (lam_q1, lam_k1, lam_q2, lam_k2)

    def run_group(x, pos, s0, seq_tiles, reps, ret_chunk, attend):
        b, l, _ = x.shape
        x2d = x.reshape(b * l, d)
        tabs = _rope_tables(pos)
        if reps[1] > 1:
            tabs = tuple(jnp.tile(t, (reps[1], 1)) for t in tabs)
        rq, rk, rv, rg, dq, dkb, dvb, dg, dk, dv = _project(x2d, g_in, w_in_b, tabs, seq_tiles, reps[0])
        ret_o, s_bd = _retain(rq, rk, rv, _state_to_blockdiag(s0), log_g, b, l, ret_chunk)
        dif_o = attend(dq, dkb, dvb)
        y = _merge(ret_o, rg, dif_o, dg, x2d, w_out_b, rng, dng, fg, 1.0 - lambda_init)
        return (y.reshape(b, l, d), _blockdiag_to_state(s_bd)[None],
                dk.reshape(1, b, l, DIFF_HEADS, DIFF_DV), dv.reshape(1, b, l, DIFF_HEADS, DIFF_DV))

    pos_p = jnp.arange(lp, dtype=F32)
    zero_state = jnp.zeros((bp, RET_HEADS, RET_DK, RET_DV), F32)
    y_p, s_p, k_p, v_p = run_group(
        x_prompt, pos_p, zero_state, lp // PROJ_TILE, (bp, 1), RET_CHUNK_PROMPT,
        lambda q, k, v: _attend_prompt(q, k, v, lams, lambda_init, bp, lp))

    pos_s = past + jnp.arange(ls, dtype=F32)
    per_tile = PROJ_TILE // ls
    ck = cache_k[0].reshape(bs, past, DIFF_WIDTH)
    cv = cache_v[0].reshape(bs, past, DIFF_WIDTH)
    y_s, s_s, k_s, v_s = run_group(
        x_sample, pos_s, state_ret[0], 1, (bs // per_tile, per_tile), ls,
        lambda q, k, v: _attend_sample(q, k, v, ck, cv, lams, lambda_init, bs, ls))

    return (y_p, y_s, s_p, s_s, k_p, v_p, k_s, v_s)
```

```python
import functools
import math

import jax
import jax.numpy as jnp
from jax import lax
from jax.experimental import pallas as pl
from jax.experimental.pallas import tpu as pltpu

F32 = jnp.float32
BF16 = jnp.bfloat16

LANES = 128
VMEM_LIMIT_BYTES = 56 * 1024 * 1024

CHUNK = 64
RET_HEADS = 8
RET_DK = 64
RET_DV = 64
RET_WIDTH = RET_HEADS * RET_DV
RET_THETA = 10000.0
DIFF_HEADS = 4
DIFF_DK = 64
DIFF_DV = 2 * DIFF_DK
DIFF_WIDTH = DIFF_HEADS * DIFF_DV
ROPE_THETA = 500000.0
ROPE_DIM = DIFF_DK // 4
EPS = 1e-6
SEG = 512
N_SEG = 8
HEAD_PAIRS = RET_WIDTH // LANES

PROJ_TILE = 512
RET_CHUNK_PROMPT = 256
ATT_TILE = PROJ_TILE
LOGIT_SCALE = DIFF_DK ** -0.5 * math.log2(math.e)
SUM_ROWS = 16


def _dot(a, b):
    return jnp.dot(a, b, preferred_element_type=F32)


def _dot_nt(a, b):
    return lax.dot_general(a, b, (((1,), (1,)), ((), ())), preferred_element_type=F32)


def _lane_iota(shape):
    return lax.broadcasted_iota(jnp.int32, shape, len(shape) - 1)


def _rope_block(xb, cos, sin, first_half, shift_up, shift_down):
    partner = jnp.where(first_half, pltpu.roll(xb, shift_up, 1), pltpu.roll(xb, shift_down, 1))
    return xb * cos + partner * sin


def _project_kernel(x_ref, g_ref, w_ref, rcos_ref, rsin_ref, dcos_ref, dsin_ref,
                    rq_ref, rk_ref, rv_ref, rg_ref, dq_ref, dkb_ref, dvb_ref, dg_ref,
                    dk_ref, dv_ref, *, transposed):
    x = x_ref[...]
    tile = x.shape[0]
    ms = jnp.mean(x * x, axis=-1, keepdims=True)
    h = (x * lax.rsqrt(ms + EPS)) * g_ref[...]
    hb = h.astype(BF16)

    lane = _lane_iota((tile, LANES))
    ret_first = (lane % RET_DK) < (RET_DK // 2)
    diff_first = (lane % DIFF_DK) < (ROPE_DIM // 2)
    rcos, rsin = rcos_ref[...], rsin_ref[...]
    dcos, dsin = dcos_ref[...], dsin_ref[...]

    def seg(i):
        return _dot(hb, w_ref[:, i * SEG:(i + 1) * SEG])

    def ret_rope(z, scale, out_ref):
        for c in range(SEG // LANES):
            sl = slice(c * LANES, (c + 1) * LANES)
            r = _rope_block(z[:, sl], rcos, rsin, ret_first, LANES - RET_DK // 2, RET_DK // 2)
            out_ref[:, sl] = (r * scale).astype(out_ref.dtype)

    def diff_rope(z, head):
        sl = slice(head * DIFF_DV, (head + 1) * DIFF_DV)
        return _rope_block(z[:, sl], dcos, dsin, diff_first, LANES - ROPE_DIM // 2, ROPE_DIM // 2)

    def head_rows(head):
        return pl.ds(head, tile, stride=DIFF_HEADS)

    ret_rope(seg(0), 1.0, rq_ref)
    ret_rope(seg(1), RET_DK ** -0.5, rk_ref)
    rv_ref[...] = seg(2).astype(BF16)
    rg_ref[...] = seg(3)
    zq, zk, zv = seg(4), seg(5), seg(6)
    for head in range(DIFF_HEADS):
        sl = slice(head * DIFF_DV, (head + 1) * DIFF_DV)
        q = diff_rope(zq, head) * LOGIT_SCALE
        k = diff_rope(zk, head)
        v = zv[:, sl]
        dk_ref[head_rows(head), :] = k
        dv_ref[head_rows(head), :] = v
        dkb_ref[:, sl] = k.astype(BF16)
        if transposed:
            dq_ref[head] = q.T.astype(BF16)
            dvb_ref[head] = v.T.astype(BF16)
        else:
            dq_ref[:, sl] = q.astype(BF16)
            dvb_ref[:, sl] = v.astype(BF16)
    dg_ref[...] = seg(7)


def _project(x2d, norm_g, w_bf16, tabs, n_seq_tiles, n_rep, transposed):
    n, d = x2d.shape
    tile = PROJ_TILE
    assert n == n_seq_tiles * n_rep * tile
    row = lambda j, r: (r * n_seq_tiles + j, 0)
    tab = lambda j, r: (j, 0)
    const = lambda j, r: (0, 0)
    out_block = pl.BlockSpec((tile, SEG), row)
    bf = jax.ShapeDtypeStruct((n, SEG), BF16)
    f32 = jax.ShapeDtypeStruct((n, SEG), F32)
    heads_block = pl.BlockSpec((tile * DIFF_HEADS, DIFF_DV), row)
    heads_f32 = jax.ShapeDtypeStruct((n * DIFF_HEADS, DIFF_DV), F32)
    if transposed:
        qv_block = pl.BlockSpec((None, DIFF_HEADS, None, DIFF_DV, tile), lambda j, r: (r, 0, j, 0, 0))
        qv_shape = jax.ShapeDtypeStruct((n_rep, DIFF_HEADS, n_seq_tiles, DIFF_DV, tile), BF16)
    else:
        qv_block, qv_shape = out_block, bf
    return pl.pallas_call(
        functools.partial(_project_kernel, transposed=transposed),
        grid=(n_seq_tiles, n_rep),
        in_specs=[
            pl.BlockSpec((tile, d), row),
            pl.BlockSpec((1, d), const),
            pl.BlockSpec((d, N_SEG * SEG), const, pipeline_mode=pl.Buffered(1)),
            pl.BlockSpec((tile, LANES), tab),
            pl.BlockSpec((tile, LANES), tab),
            pl.BlockSpec((tile, LANES), tab),
            pl.BlockSpec((tile, LANES), tab),
        ],
        out_specs=[out_block, out_block, out_block, out_block, qv_block, out_block, qv_block, out_block,
                   heads_block, heads_block],
        out_shape=[bf, bf, bf, f32, qv_shape, bf, qv_shape, f32, heads_f32, heads_f32],
        compiler_params=pltpu.CompilerParams(
            dimension_semantics=("arbitrary", "arbitrary"), vmem_limit_bytes=VMEM_LIMIT_BYTES),
        name="project",
    )(x2d, norm_g, w_bf16, *tabs)


def _retain_kernel(q_ref, k_ref, v_ref, s0_ref, dec_ref, qdec_ref, kdec_ref, gc_ref,
                   o_ref, sout_ref, s_scr):
    j = pl.program_id(1)

    @pl.when(j == 0)
    def _():
        s_scr[...] = s0_ref[...]

    c = q_ref.shape[0]
    lane = _lane_iota((c, LANES))
    low = lane < RET_DK
    row_i = lax.broadcasted_iota(jnp.int32, (LANES, LANES), 0)
    col_i = lax.broadcasted_iota(jnp.int32, (LANES, LANES), 1)
    same_head = (row_i < RET_DK) == (col_i < RET_DK)

    for p in range(HEAD_PAIRS):
        sl = slice(p * LANES, (p + 1) * LANES)
        q2, k2, v2 = q_ref[:, sl], k_ref[:, sl], v_ref[:, sl]
        s = s_scr[p]
        cross = _dot(q2, s.astype(BF16)) * qdec_ref[:, sl]
        zero = jnp.zeros_like(q2)
        sa = _dot_nt(jnp.where(low, q2, zero), k2) * dec_ref[2 * p]
        sb = _dot_nt(jnp.where(low, zero, q2), k2) * dec_ref[2 * p + 1]
        inner = jnp.where(low, _dot(sa.astype(BF16), v2), _dot(sb.astype(BF16), v2))
        o_ref[:, sl] = inner + cross
        kd = k2.astype(F32) * kdec_ref[:, sl]
        upd = _dot(kd.T.astype(BF16), v2)
        s_scr[p] = gc_ref[:, sl] * s + jnp.where(same_head, upd, 0.0)

    @pl.when(j == pl.num_programs(1) - 1)
    def _():
        sout_ref[...] = s_scr[...]


def _retention_tables(log_g, c):
    idx = jnp.arange(c, dtype=F32)
    rel = idx[:, None] - idx[None, :]
    dec = jnp.where((rel >= 0)[None], jnp.exp(log_g[:, None, None] * jnp.maximum(rel, 0.0)[None]), 0.0)
    per_lane = lambda t: jnp.repeat(t, RET_DV, axis=-1)
    qdec = per_lane(jnp.exp(log_g[None, :] * (idx[:, None] + 1.0)))
    kdec = per_lane(jnp.exp(log_g[None, :] * (c - 1.0 - idx[:, None])))
    gc = per_lane(jnp.exp(log_g * c)[None, :])
    return dec, qdec, kdec, gc


def _retain(q, k, v, s0_bd, log_g, batch, seq, c):
    n_chunks = seq // c
    dec, qdec, kdec, gc = _retention_tables(log_g, c)
    row = lambda b, j: (b * n_chunks + j, 0)
    blk = pl.BlockSpec((c, RET_WIDTH), row)
    st = pl.BlockSpec((None, HEAD_PAIRS, LANES, LANES), lambda b, j: (b, 0, 0, 0))
    return pl.pallas_call(
        _retain_kernel,
        grid=(batch, n_chunks),
        in_specs=[
            blk, blk, blk, st,
            pl.BlockSpec((RET_HEADS, c, c), lambda b, j: (0, 0, 0)),
            pl.BlockSpec((c, RET_WIDTH), lambda b, j: (0, 0)),
            pl.BlockSpec((c, RET_WIDTH), lambda b, j: (0, 0)),
            pl.BlockSpec((1, RET_WIDTH), lambda b, j: (0, 0)),
        ],
        out_specs=[blk, st],
        out_shape=[jax.ShapeDtypeStruct((batch * seq, RET_WIDTH), F32),
                   jax.ShapeDtypeStruct((batch, HEAD_PAIRS, LANES, LANES), F32)],
        scratch_shapes=[pltpu.VMEM((HEAD_PAIRS, LANES, LANES), F32)],
        compiler_params=pltpu.CompilerParams(
            dimension_semantics=("arbitrary", "arbitrary"), vmem_limit_bytes=VMEM_LIMIT_BYTES),
        name="retain",
    )(q, k, v, s0_bd, dec, qdec, kdec, gc)


def _state_to_blockdiag(s):
    b = s.shape[0]
    s = s.reshape(b, HEAD_PAIRS, 2, RET_DK, RET_DV)
    z = jnp.zeros_like(s[:, :, 0])
    top = jnp.concatenate([s[:, :, 0], z], axis=-1)
    bot = jnp.concatenate([z, s[:, :, 1]], axis=-1)
    return jnp.concatenate([top, bot], axis=-2)


def _blockdiag_to_state(sb):
    b = sb.shape[0]
    a = sb[:, :, :RET_DK, :RET_DV]
    d = sb[:, :, RET_DK:, RET_DV:]
    return jnp.stack([a, d], axis=2).reshape(b, RET_HEADS, RET_DK, RET_DV)


def _lambda_value(lq1_ref, lk1_ref, lq2_ref, lk2_ref, lambda_init):
    a = jnp.sum(lq1_ref[...] * lk1_ref[...], axis=-1, keepdims=True)
    b = jnp.sum(lq2_ref[...] * lk2_ref[...], axis=-1, keepdims=True)
    return jnp.exp(a) - jnp.exp(b) + lambda_init


def _attend_prompt_kernel(lq1_ref, lk1_ref, lq2_ref, lk2_ref, qt_ref, k_ref, vt_ref, o_ref,
                          sa_scr, sb_scr, mxa_scr, mxb_scr, qm_scr, p_scr, m_scr, acc_scr,
                          *, lambda_init):
    i = pl.program_id(2)
    t = ATT_TILE
    qt = qt_ref[...]
    dim = lax.broadcasted_iota(jnp.int32, qt.shape, 0)
    zero = jnp.zeros_like(qt)
    qm_scr[0] = jnp.where(dim < DIFF_DK, qt, zero)
    qm_scr[1] = jnp.where(dim < DIFF_DK, zero, qt)

    m_scr[...] = jnp.full(m_scr.shape, -jnp.inf, F32)
    acc_scr[...] = jnp.zeros(acc_scr.shape, F32)
    ones_rows = jnp.ones((SUM_ROWS, t), BF16)

    def scores(j, s_buf, mx_buf, masked):
        k = k_ref[pl.ds(pl.multiple_of(j * t, t), t), :]
        if masked:
            kc = lax.broadcasted_iota(jnp.int32, (t, t), 0) // CHUNK
            qc = lax.broadcasted_iota(jnp.int32, (t, t), 1) // CHUNK
            visible = kc <= qc
        for c in range(2):
            s = _dot(k, qm_scr[c])
            if masked:
                s = jnp.where(visible, s, -jnp.inf)
            s_buf[c] = s
            mx_buf[c] = jnp.max(s, axis=0, keepdims=True)

    def absorb(j, s_buf, mx_buf):
        vt = jnp.concatenate([vt_ref[j], ones_rows], axis=0)
        for c in range(2):
            m_old = m_scr[c]
            m_new = jnp.maximum(m_old, mx_buf[c])
            alpha = jnp.exp2(m_old - m_new)
            m_scr[c] = m_new
            for b in range(t // LANES):
                sl = slice(b * LANES, (b + 1) * LANES)
                p_scr[c, :, sl] = jnp.exp2(s_buf[c, :, sl] - m_new[:, sl]).astype(BF16)
            acc_scr[c] = alpha * acc_scr[c] + _dot(vt, p_scr[c])

    a_bufs, b_bufs = (sa_scr, mxa_scr), (sb_scr, mxb_scr)

    @pl.when(i > 0)
    def _():
        scores(0, *a_bufs, False)

    def pair(u, carry):
        scores(2 * u + 1, *b_bufs, False)
        absorb(2 * u, *a_bufs)
        scores(2 * u + 2, *a_bufs, False)
        absorb(2 * u + 1, *b_bufs)
        return carry

    lax.fori_loop(0, i // 2, pair, 0)

    @pl.when(i % 2 == 1)
    def _():
        scores(i, *b_bufs, True)
        absorb(i - 1, *a_bufs)

    @pl.when(i % 2 == 0)
    def _():
        scores(i, *b_bufs, True)

    absorb(i, *b_bufs)

    lam = _lambda_value(lq1_ref, lk1_ref, lq2_ref, lk2_ref, lambda_init)
    def normalised(c):
        return acc_scr[c, :DIFF_DV, :] * (1.0 / acc_scr[c, DIFF_DV:DIFF_DV + 1, :])

    o_ref[...] = (normalised(0) - lam * normalised(1)).T


def _attend_prompt(qt, k, vt, lams, lambda_init, batch, seq):
    t = ATT_TILE
    assert t % CHUNK == 0 and seq % t == 0
    nq = seq // t
    lam_spec = pl.BlockSpec((1, DIFF_DK), lambda b, h, i: (0, 0))
    q_spec = pl.BlockSpec((None, None, None, DIFF_DV, t), lambda b, h, i: (b, h, i, 0, 0))
    k_spec = pl.BlockSpec((seq, DIFF_DV), lambda b, h, i: (b, h))
    v_spec = pl.BlockSpec((None, None, nq, DIFF_DV, t), lambda b, h, i: (b, h, 0, 0, 0))
    o_spec = pl.BlockSpec((t, DIFF_DV), lambda b, h, i: (b * nq + i, h))
    return pl.pallas_call(
        functools.partial(_attend_prompt_kernel, lambda_init=lambda_init),
        grid=(batch, DIFF_HEADS, nq),
        in_specs=[lam_spec] * 4 + [q_spec, k_spec, v_spec],
        out_specs=o_spec,
        out_shape=jax.ShapeDtypeStruct((batch * seq, DIFF_WIDTH), F32),
        scratch_shapes=[pltpu.VMEM((2, t, t), F32), pltpu.VMEM((2, t, t), F32),
                        pltpu.VMEM((2, 1, t), F32), pltpu.VMEM((2, 1, t), F32),
                        pltpu.VMEM((2, DIFF_DV, t), BF16),
                        pltpu.VMEM((2, t, t), BF16),
                        pltpu.VMEM((2, 1, t), F32),
                        pltpu.VMEM((2, DIFF_DV + SUM_ROWS, t), F32)],
        compiler_params=pltpu.CompilerParams(
            dimension_semantics=("arbitrary", "arbitrary", "arbitrary"),
            vmem_limit_bytes=VMEM_LIMIT_BYTES),
        name="attend_prompt",
    )(*lams, qt, k, vt)


def _attend_sample_kernel(lq1_ref, lk1_ref, lq2_ref, lk2_ref, q_ref, kn_ref, vn_ref, kc_ref, vc_ref,
                          o_ref, *, lambda_init):
    lam = _lambda_value(lq1_ref, lk1_ref, lq2_ref, lk2_ref, lambda_init)
    lane = _lane_iota((q_ref.shape[0], LANES))
    for h in range(DIFF_HEADS):
        sl = slice(h * DIFF_DV, (h + 1) * DIFF_DV)
        q, kn, vn = q_ref[:, sl], kn_ref[:, sl], vn_ref[:, sl]
        past = kc_ref.shape[0] // DIFF_HEADS
        kc = kc_ref[pl.ds(h, past, stride=DIFF_HEADS), :].astype(BF16)
        vc = vc_ref[pl.ds(h, past, stride=DIFF_HEADS), :].astype(BF16)
        zero = jnp.zeros_like(q)
        w_c = None
        w_n = None
        for c in range(2):
            qc = jnp.where(lane < DIFF_DK, q, zero) if c == 0 else jnp.where(lane < DIFF_DK, zero, q)
            s_c = _dot_nt(qc, kc)
            s_n = _dot_nt(qc, kn)
            m = jnp.maximum(jnp.max(s_c, axis=-1, keepdims=True), jnp.max(s_n, axis=-1, keepdims=True))
            p_c = jnp.exp2(s_c - m)
            p_n = jnp.exp2(s_n - m)
            inv = 1.0 / (jnp.sum(p_c, axis=-1, keepdims=True) + jnp.sum(p_n, axis=-1, keepdims=True))
            if c == 0:
                w_c, w_n = p_c * inv, p_n * inv
            else:
                w_c, w_n = w_c - lam * (p_c * inv), w_n - lam * (p_n * inv)
        o_ref[:, sl] = _dot(w_c.astype(BF16), vc) + _dot(w_n.astype(BF16), vn)


def _attend_sample(q, kn, vn, cache_k, cache_v, lams, lambda_init, batch, seq):
    rows = cache_k.shape[0] // batch
    lam_spec = pl.BlockSpec((1, DIFF_DK), lambda b: (0, 0))
    tok = pl.BlockSpec((seq, DIFF_WIDTH), lambda b: (b, 0))
    cache = pl.BlockSpec((rows, DIFF_DV), lambda b: (b, 0))
    return pl.pallas_call(
        functools.partial(_attend_sample_kernel, lambda_init=lambda_init),
        grid=(batch,),
        in_specs=[lam_spec] * 4 + [tok, tok, tok, cache, cache],
        out_specs=tok,
        out_shape=jax.ShapeDtypeStruct((batch * seq, DIFF_WIDTH), F32),
        compiler_params=pltpu.CompilerParams(
            dimension_semantics=("arbitrary",), vmem_limit_bytes=VMEM_LIMIT_BYTES),
        name="attend_sample",
    )(*lams, q, kn, vn, cache_k, cache_v)


def _silu(g):
    return g * (1.0 / (1.0 + jnp.exp(-g)))


def _merge_kernel(ret_ref, rg_ref, dif_ref, dg_ref, x_ref, w_ref, rng_ref, dng_ref, fg_ref, y_ref,
                  *, diff_scale):
    tile = x_ref.shape[0]
    lane = _lane_iota((tile, LANES))
    low = lane < RET_DV
    ret_parts, dif_parts = [], []
    for c in range(RET_WIDTH // LANES):
        sl = slice(c * LANES, (c + 1) * LANES)
        r = ret_ref[:, sl]
        sq = r * r
        ss_a = jnp.sum(jnp.where(low, sq, 0.0), axis=-1, keepdims=True)
        ss_b = jnp.sum(jnp.where(low, 0.0, sq), axis=-1, keepdims=True)
        ms = jnp.where(low, ss_a, ss_b) * (1.0 / RET_DV)
        normed = (r * lax.rsqrt(ms + EPS)) * rng_ref[:, sl]
        ret_parts.append((_silu(rg_ref[:, sl]) * normed).astype(BF16))
    for c in range(DIFF_HEADS):
        sl = slice(c * DIFF_DV, (c + 1) * DIFF_DV)
        d = dif_ref[:, sl]
        ms = jnp.mean(d * d, axis=-1, keepdims=True)
        normed = ((d * lax.rsqrt(ms + EPS)) * dng_ref[:, sl]) * diff_scale
        dif_parts.append((_silu(dg_ref[:, sl]) * normed).astype(BF16))
    mixed_ret = jnp.concatenate(ret_parts, axis=-1)
    mixed_dif = jnp.concatenate(dif_parts, axis=-1)
    out = _dot(mixed_ret, w_ref[:RET_WIDTH, :]) + _dot(mixed_dif, w_ref[RET_WIDTH:, :])
    xo = x_ref[...] + out
    ms = jnp.mean(xo * xo, axis=-1, keepdims=True)
    y_ref[...] = (xo * lax.rsqrt(ms + EPS)) * fg_ref[...]


def _merge(ret_o, rg, dif_o, dg, x2d, w_out_bf16, ret_norm_g, diff_norm_g, final_g, diff_scale):
    n, d = x2d.shape
    tile = PROJ_TILE
    row = lambda i: (i, 0)
    const = lambda i: (0, 0)
    half = pl.BlockSpec((tile, SEG), row)
    return pl.pallas_call(
        functools.partial(_merge_kernel, diff_scale=diff_scale),
        grid=(n // tile,),
        in_specs=[half, half, half, half,
                  pl.BlockSpec((tile, d), row),
                  pl.BlockSpec((RET_WIDTH + DIFF_WIDTH, d), const, pipeline_mode=pl.Buffered(1)),
                  pl.BlockSpec((1, RET_WIDTH), const),
                  pl.BlockSpec((1, DIFF_WIDTH), const),
                  pl.BlockSpec((1, d), const)],
        out_specs=pl.BlockSpec((tile, d), row),
        out_shape=jax.ShapeDtypeStruct((n, d), F32),
        compiler_params=pltpu.CompilerParams(
            dimension_semantics=("arbitrary",), vmem_limit_bytes=VMEM_LIMIT_BYTES),
        name="merge",
    )(ret_o, rg, dif_o, dg, x2d, w_out_bf16, ret_norm_g, diff_norm_g, final_g)


def _rope_tables(pos):
    def tables(dim, theta, block):
        inv_freq = 1.0 / (theta ** (jnp.arange(0, dim, 2, dtype=F32) / dim))
        ang = pos[:, None] * inv_freq[None, :]
        cos, sin = jnp.cos(ang), jnp.sin(ang)
        pad = block - dim
        ones = jnp.ones((pos.shape[0], pad), F32)
        zeros = jnp.zeros((pos.shape[0], pad), F32)
        cos_b = jnp.concatenate([cos, cos, ones], axis=-1)
        sin_b = jnp.concatenate([-sin, sin, zeros], axis=-1)
        reps = LANES // block
        return jnp.tile(cos_b, (1, reps)), jnp.tile(sin_b, (1, reps))

    rcos, rsin = tables(RET_DK, RET_THETA, RET_DK)
    dcos, dsin = tables(ROPE_DIM, ROPE_THETA, DIFF_DK)
    return rcos, rsin, dcos, dsin


def kernel(x_prompt, x_sample, cache_k, cache_v, state_ret, norm_g, w_in, w_out, ret_norm_g, diff_norm_g,
           lam_q1, lam_k1, lam_q2, lam_k2, final_norm_g):
    depth = w_in.shape[0]
    assert depth == 1, "single-layer trunk"
    bp, lp, d = x_prompt.shape
    bs, ls, _ = x_sample.shape
    past = cache_k.shape[2]
    lambda_init = 0.8 - 0.6 * math.exp(-0.3 * 0)
    log_g = jnp.log1p(-jnp.exp2(-5.0 - jnp.arange(RET_HEADS, dtype=F32)))

    w_in_b = w_in[0].astype(BF16)
    w_out_b = w_out[0].astype(BF16)
    g_in = norm_g[0][None, :]
    rng = ret_norm_g[0][None, :]
    dng = jnp.tile(diff_norm_g[0], DIFF_HEADS)[None, :]
    fg = final_norm_g[None, :]
    lams = (lam_q1, lam_k1, lam_q2, lam_k2)

    def run_group(x, pos, s0, seq_tiles, reps, ret_chunk, transposed, attend):
        b, l, _ = x.shape
        x2d = x.reshape(b * l, d)
        tabs = _rope_tables(pos)
        if reps[1] > 1:
            tabs = tuple(jnp.tile(t, (reps[1], 1)) for t in tabs)
        rq, rk, rv, rg, dq, dkb, dvb, dg, dk, dv = _project(
            x2d, g_in, w_in_b, tabs, seq_tiles, reps[0], transposed)
        ret_o, s_bd = _retain(rq, rk, rv, _state_to_blockdiag(s0), log_g, b, l, ret_chunk)
        dif_o = attend(dq, dkb, dvb)
        y = _merge(ret_o, rg, dif_o, dg, x2d, w_out_b, rng, dng, fg, 1.0 - lambda_init)
        return (y.reshape(b, l, d), _blockdiag_to_state(s_bd)[None],
                dk.reshape(1, b, l, DIFF_HEADS, DIFF_DV), dv.reshape(1, b, l, DIFF_HEADS, DIFF_DV))

    pos_p = jnp.arange(lp, dtype=F32)
    zero_state = jnp.zeros((bp, RET_HEADS, RET_DK, RET_DV), F32)
    y_p, s_p, k_p, v_p = run_group(
        x_prompt, pos_p, zero_state, lp // PROJ_TILE, (bp, 1), RET_CHUNK_PROMPT, True,
        lambda qt, k, vt: _attend_prompt(qt, k, vt, lams, lambda_init, bp, lp))

    pos_s = past + jnp.arange(ls, dtype=F32)
    per_tile = PROJ_TILE // ls
    ck = cache_k[0].reshape(bs * past * DIFF_HEADS, DIFF_DV)
    cv = cache_v[0].reshape(bs * past * DIFF_HEADS, DIFF_DV)
    y_s, s_s, k_s, v_s = run_group(
        x_sample, pos_s, state_ret[0], 1, (bs // per_tile, per_tile), ls, False,
        lambda q, k, v: _attend_sample(q, k, v, ck, cv, lams, lambda_init, bs, ls))

    return (y_p, y_s, s_p, s_s, k_p, v_p, k_s, v_s)
```

```python
import functools
import math

import jax
import jax.numpy as jnp
from jax import lax
from jax.experimental import pallas as pl
from jax.experimental.pallas import tpu as pltpu

F32 = jnp.float32
BF16 = jnp.bfloat16

LANES = 128
VMEM_LIMIT_BYTES = 56 * 1024 * 1024

CHUNK = 64
RET_HEADS = 8
RET_DK = 64
RET_DV = 64
RET_WIDTH = RET_HEADS * RET_DV
RET_THETA = 10000.0
DIFF_HEADS = 4
DIFF_DK = 64
DIFF_DV = 2 * DIFF_DK
DIFF_WIDTH = DIFF_HEADS * DIFF_DV
ROPE_THETA = 500000.0
ROPE_DIM = DIFF_DK // 4
EPS = 1e-6
SEG = 512
N_SEG = 8
HEAD_PAIRS = RET_WIDTH // LANES

PROJ_TILE = 512
RET_CHUNK_PROMPT = 256
ATT_TILE = PROJ_TILE
LOGIT_SCALE = DIFF_DK ** -0.5 * math.log2(math.e)
SUM_ROWS = 16


def _dot(a, b):
    return jnp.dot(a, b, preferred_element_type=F32)


def _dot_nt(a, b):
    return lax.dot_general(a, b, (((1,), (1,)), ((), ())), preferred_element_type=F32)


def _lane_iota(shape):
    return lax.broadcasted_iota(jnp.int32, shape, len(shape) - 1)


def _rope_block(xb, cos, sin, first_half, shift_up, shift_down):
    partner = jnp.where(first_half, pltpu.roll(xb, shift_up, 1), pltpu.roll(xb, shift_down, 1))
    return xb * cos + partner * sin


def _project_kernel(x_ref, g_ref, w_ref, rcos_ref, rsin_ref, dcos_ref, dsin_ref,
                    rq_ref, rk_ref, rv_ref, rg_ref, dq_ref, dkb_ref, dvb_ref, dg_ref,
                    dk_ref, dv_ref, *, transposed):
    x = x_ref[...]
    tile = x.shape[0]
    ms = jnp.mean(x * x, axis=-1, keepdims=True)
    h = (x * lax.rsqrt(ms + EPS)) * g_ref[...]
    hb = h.astype(BF16)

    lane = _lane_iota((tile, LANES))
    ret_first = (lane % RET_DK) < (RET_DK // 2)
    diff_first = (lane % DIFF_DK) < (ROPE_DIM // 2)
    rcos, rsin = rcos_ref[...], rsin_ref[...]
    dcos, dsin = dcos_ref[...], dsin_ref[...]

    def seg(i):
        return _dot(hb, w_ref[:, i * SEG:(i + 1) * SEG])

    def ret_rope(z, scale, out_ref):
        for c in range(SEG // LANES):
            sl = slice(c * LANES, (c + 1) * LANES)
            r = _rope_block(z[:, sl], rcos, rsin, ret_first, LANES - RET_DK // 2, RET_DK // 2)
            out_ref[:, sl] = (r * scale).astype(out_ref.dtype)

    def diff_rope(z, head):
        sl = slice(head * DIFF_DV, (head + 1) * DIFF_DV)
        return _rope_block(z[:, sl], dcos, dsin, diff_first, LANES - ROPE_DIM // 2, ROPE_DIM // 2)

    def head_rows(head):
        return pl.ds(head, tile, stride=DIFF_HEADS)

    ret_rope(seg(0), 1.0, rq_ref)
    ret_rope(seg(1), RET_DK ** -0.5, rk_ref)
    rv_ref[...] = seg(2).astype(BF16)
    rg_ref[...] = seg(3)
    zq, zk, zv = seg(4), seg(5), seg(6)
    for head in range(DIFF_HEADS):
        sl = slice(head * DIFF_DV, (head + 1) * DIFF_DV)
        q = diff_rope(zq, head) * LOGIT_SCALE
        k = diff_rope(zk, head)
        v = zv[:, sl]
        dk_ref[head_rows(head), :] = k
        dv_ref[head_rows(head), :] = v
        dkb_ref[:, sl] = k.astype(BF16)
        if transposed:
            dq_ref[head] = q.T.astype(BF16)
            dvb_ref[head] = v.T.astype(BF16)
        else:
            dq_ref[:, sl] = q.astype(BF16)
            dvb_ref[:, sl] = v.astype(BF16)
    dg_ref[...] = seg(7)


def _project(x2d, norm_g, w_bf16, tabs, n_seq_tiles, n_rep, transposed):
    n, d = x2d.shape
    tile = PROJ_TILE
    assert n == n_seq_tiles * n_rep * tile
    row = lambda j, r: (r * n_seq_tiles + j, 0)
    tab = lambda j, r: (j, 0)
    const = lambda j, r: (0, 0)
    out_block = pl.BlockSpec((tile, SEG), row)
    bf = jax.ShapeDtypeStruct((n, SEG), BF16)
    f32 = jax.ShapeDtypeStruct((n, SEG), F32)
    heads_block = pl.BlockSpec((tile * DIFF_HEADS, DIFF_DV), row)
    heads_f32 = jax.ShapeDtypeStruct((n * DIFF_HEADS, DIFF_DV), F32)
    if transposed:
        qv_block = pl.BlockSpec((None, DIFF_HEADS, None, DIFF_DV, tile), lambda j, r: (r, 0, j, 0, 0))
        qv_shape = jax.ShapeDtypeStruct((n_rep, DIFF_HEADS, n_seq_tiles, DIFF_DV, tile), BF16)
    else:
        qv_block, qv_shape = out_block, bf
    return pl.pallas_call(
        functools.partial(_project_kernel, transposed=transposed),
        grid=(n_seq_tiles, n_rep),
        in_specs=[
            pl.BlockSpec((tile, d), row),
            pl.BlockSpec((1, d), const),
            pl.BlockSpec((d, N_SEG * SEG), const, pipeline_mode=pl.Buffered(1)),
            pl.BlockSpec((tile, LANES), tab),
            pl.BlockSpec((tile, LANES), tab),
            pl.BlockSpec((tile, LANES), tab),
            pl.BlockSpec((tile, LANES), tab),
        ],
        out_specs=[out_block, out_block, out_block, out_block, qv_block, out_block, qv_block, out_block,
                   heads_block, heads_block],
        out_shape=[bf, bf, bf, f32, qv_shape, bf, qv_shape, f32, heads_f32, heads_f32],
        compiler_params=pltpu.CompilerParams(
            dimension_semantics=("arbitrary", "arbitrary"), vmem_limit_bytes=VMEM_LIMIT_BYTES),
        name="project",
    )(x2d, norm_g, w_bf16, *tabs)


def _retain_kernel(q_ref, k_ref, v_ref, s0_ref, dec_ref, qdec_ref, kdec_ref, gc_ref,
                   o_ref, sout_ref, s_scr):
    j = pl.program_id(1)

    @pl.when(j == 0)
    def _():
        s_scr[...] = s0_ref[...]

    c = q_ref.shape[0]
    lane = _lane_iota((c, LANES))
    low = lane < RET_DK
    row_i = lax.broadcasted_iota(jnp.int32, (LANES, LANES), 0)
    col_i = lax.broadcasted_iota(jnp.int32, (LANES, LANES), 1)
    same_head = (row_i < RET_DK) == (col_i < RET_DK)

    for p in range(HEAD_PAIRS):
        sl = slice(p * LANES, (p + 1) * LANES)
        q2, k2, v2 = q_ref[:, sl], k_ref[:, sl], v_ref[:, sl]
        s = s_scr[p]
        cross = _dot(q2, s.astype(BF16)) * qdec_ref[:, sl]
        zero = jnp.zeros_like(q2)
        sa = _dot_nt(jnp.where(low, q2, zero), k2) * dec_ref[2 * p]
        sb = _dot_nt(jnp.where(low, zero, q2), k2) * dec_ref[2 * p + 1]
        inner = jnp.where(low, _dot(sa.astype(BF16), v2), _dot(sb.astype(BF16), v2))
        o_ref[:, sl] = inner + cross
        kd = k2.astype(F32) * kdec_ref[:, sl]
        upd = _dot(kd.T.astype(BF16), v2)
        s_scr[p] = gc_ref[:, sl] * s + jnp.where(same_head, upd, 0.0)

    @pl.when(j == pl.num_programs(1) - 1)
    def _():
        sout_ref[...] = s_scr[...]


def _retention_tables(log_g, c):
    idx = jnp.arange(c, dtype=F32)
    rel = idx[:, None] - idx[None, :]
    dec = jnp.where((rel >= 0)[None], jnp.exp(log_g[:, None, None] * jnp.maximum(rel, 0.0)[None]), 0.0)
    per_lane = lambda t: jnp.repeat(t, RET_DV, axis=-1)
    qdec = per_lane(jnp.exp(log_g[None, :] * (idx[:, None] + 1.0)))
    kdec = per_lane(jnp.exp(log_g[None, :] * (c - 1.0 - idx[:, None])))
    gc = per_lane(jnp.exp(log_g * c)[None, :])
    return dec, qdec, kdec, gc


def _retain(q, k, v, s0_bd, log_g, batch, seq, c):
    n_chunks = seq // c
    dec, qdec, kdec, gc = _retention_tables(log_g, c)
    row = lambda b, j: (b * n_chunks + j, 0)
    blk = pl.BlockSpec((c, RET_WIDTH), row)
    st = pl.BlockSpec((None, HEAD_PAIRS, LANES, LANES), lambda b, j: (b, 0, 0, 0))
    return pl.pallas_call(
        _retain_kernel,
        grid=(batch, n_chunks),
        in_specs=[
            blk, blk, blk, st,
            pl.BlockSpec((RET_HEADS, c, c), lambda b, j: (0, 0, 0)),
            pl.BlockSpec((c, RET_WIDTH), lambda b, j: (0, 0)),
            pl.BlockSpec((c, RET_WIDTH), lambda b, j: (0, 0)),
            pl.BlockSpec((1, RET_WIDTH), lambda b, j: (0, 0)),
        ],
        out_specs=[blk, st],
        out_shape=[jax.ShapeDtypeStruct((batch * seq, RET_WIDTH), F32),
                   jax.ShapeDtypeStruct((batch, HEAD_PAIRS, LANES, LANES), F32)],
        scratch_shapes=[pltpu.VMEM((HEAD_PAIRS, LANES, LANES), F32)],
        compiler_params=pltpu.CompilerParams(
            dimension_semantics=("arbitrary", "arbitrary"), vmem_limit_bytes=VMEM_LIMIT_BYTES),
        name="retain",
    )(q, k, v, s0_bd, dec, qdec, kdec, gc)


def _state_to_blockdiag(s):
    b = s.shape[0]
    s = s.reshape(b, HEAD_PAIRS, 2, RET_DK, RET_DV)
    z = jnp.zeros_like(s[:, :, 0])
    top = jnp.concatenate([s[:, :, 0], z], axis=-1)
    bot = jnp.concatenate([z, s[:, :, 1]], axis=-1)
    return jnp.concatenate([top, bot], axis=-2)


def _blockdiag_to_state(sb):
    b = sb.shape[0]
    a = sb[:, :, :RET_DK, :RET_DV]
    d = sb[:, :, RET_DK:, RET_DV:]
    return jnp.stack([a, d], axis=2).reshape(b, RET_HEADS, RET_DK, RET_DV)


def _lambda_value(lq1_ref, lk1_ref, lq2_ref, lk2_ref, lambda_init):
    a = jnp.sum(lq1_ref[...] * lk1_ref[...], axis=-1, keepdims=True)
    b = jnp.sum(lq2_ref[...] * lk2_ref[...], axis=-1, keepdims=True)
    return jnp.exp(a) - jnp.exp(b) + lambda_init


def _attend_prompt_kernel(lq1_ref, lk1_ref, lq2_ref, lk2_ref, qt_ref, k_ref, vt_ref, o_ref,
                          *scratch, lambda_init):
    chains = [(c, h) for h in range(2) for c in range(2)]
    n = len(chains)
    groups = [scratch[g * n:(g + 1) * n] for g in range(len(scratch) // n)]
    qm_scr, m_scr, acc_scr = groups[0], groups[1], groups[2]
    s_scr, p_scr = groups[3:5], groups[5:7]
    mx_scr, al_scr = groups[7:9], groups[9:11]

    i = pl.program_id(2)
    t = ATT_TILE
    half = t // 2
    qt = qt_ref[...]
    dim = lax.broadcasted_iota(jnp.int32, qt.shape, 0)
    zero = jnp.zeros_like(qt)
    q_map = (jnp.where(dim < DIFF_DK, qt, zero), jnp.where(dim < DIFF_DK, zero, qt))
    for x, (c, h) in enumerate(chains):
        qm_scr[x][...] = q_map[c][:, h * half:(h + 1) * half]
        m_scr[x][...] = jnp.full(m_scr[x].shape, -jnp.inf, F32)
        acc_scr[x][...] = jnp.zeros(acc_scr[x].shape, F32)
    ones_rows = jnp.ones((SUM_ROWS, t), BF16)

    def key_tile(j):
        return k_ref[pl.ds(pl.multiple_of(j * t, t), t), :]

    def value_tile(j):
        return jnp.concatenate([vt_ref[j], ones_rows], axis=0)

    def qk(k, x, parity, masked):
        s = _dot(k, qm_scr[x][...])
        if masked:
            h = chains[x][1]
            kc = lax.broadcasted_iota(jnp.int32, (t, half), 0) // CHUNK
            qc = (lax.broadcasted_iota(jnp.int32, (t, half), 1) + h * half) // CHUNK
            s = jnp.where(kc <= qc, s, -jnp.inf)
        s_scr[parity][x][...] = s
        mx_scr[parity][x][...] = jnp.max(s, axis=0, keepdims=True)

    def sm(x, parity):
        s_buf, p_buf = s_scr[parity][x], p_scr[parity][x]
        m_old = m_scr[x][...]
        m_new = jnp.maximum(m_old, mx_scr[parity][x][...])
        m_scr[x][...] = m_new
        al_scr[parity][x][...] = jnp.exp2(m_old - m_new)
        for b in range(half // LANES):
            sl = slice(b * LANES, (b + 1) * LANES)
            p_buf[:, sl] = jnp.exp2(s_buf[:, sl] - m_new[:, sl]).astype(BF16)

    def pv(vt, x, parity):
        acc_scr[x][...] = (al_scr[parity][x][...] * acc_scr[x][...]
                           + _dot(vt, p_scr[parity][x][...]))

    def stage(j_sm, par_sm, j_qk, par_qk, masked):
        k = None if j_qk is None else key_tile(j_qk)
        for x in range(n):
            if j_sm is not None:
                sm(x, par_sm)
            if k is not None:
                qk(k, x, par_qk, masked)
        if j_sm is not None:
            vt = value_tile(j_sm)
            for x in range(n):
                pv(vt, x, par_sm)

    @pl.when(i > 0)
    def _():
        stage(None, None, 0, 0, False)

    def pair(u, carry):
        stage(2 * u, 0, 2 * u + 1, 1, False)
        stage(2 * u + 1, 1, 2 * u + 2, 0, False)
        return carry

    lax.fori_loop(0, i // 2, pair, 0)

    @pl.when(i % 2 == 1)
    def _():
        stage(i - 1, 0, i, 1, True)

    @pl.when(i % 2 == 0)
    def _():
        stage(None, None, i, 1, True)

    stage(i, 1, None, None, False)

    lam = _lambda_value(lq1_ref, lk1_ref, lq2_ref, lk2_ref, lambda_init)
    for h in range(2):
        norm = []
        for c in range(2):
            acc = acc_scr[chains.index((c, h))]
            norm.append(acc[:DIFF_DV, :] * (1.0 / acc[DIFF_DV:DIFF_DV + 1, :]))
        o_ref[h * half:(h + 1) * half, :] = (norm[0] - lam * norm[1]).T


def _attend_prompt(qt, k, vt, lams, lambda_init, batch, seq):
    t = ATT_TILE
    assert t % CHUNK == 0 and seq % t == 0
    nq = seq // t
    n_chains = 4
    lam_spec = pl.BlockSpec((1, DIFF_DK), lambda b, h, i: (0, 0))
    q_spec = pl.BlockSpec((None, None, None, DIFF_DV, t), lambda b, h, i: (b, h, i, 0, 0))
    k_spec = pl.BlockSpec((seq, DIFF_DV), lambda b, h, i: (b, h))
    v_spec = pl.BlockSpec((None, None, nq, DIFF_DV, t), lambda b, h, i: (b, h, 0, 0, 0))
    o_spec = pl.BlockSpec((t, DIFF_DV), lambda b, h, i: (b * nq + i, h))
    return pl.pallas_call(
        functools.partial(_attend_prompt_kernel, lambda_init=lambda_init),
        grid=(batch, DIFF_HEADS, nq),
        in_specs=[lam_spec] * 4 + [q_spec, k_spec, v_spec],
        out_specs=o_spec,
        out_shape=jax.ShapeDtypeStruct((batch * seq, DIFF_WIDTH), F32),
        scratch_shapes=(
            [pltpu.VMEM((DIFF_DV, t // 2), BF16)] * n_chains
            + [pltpu.VMEM((1, t // 2), F32)] * n_chains
            + [pltpu.VMEM((DIFF_DV + SUM_ROWS, t // 2), F32)] * n_chains
            + [pltpu.VMEM((t, t // 2), F32)] * (2 * n_chains)
            + [pltpu.VMEM((t, t // 2), BF16)] * (2 * n_chains)
            + [pltpu.VMEM((1, t // 2), F32)] * (4 * n_chains)),
        compiler_params=pltpu.CompilerParams(
            dimension_semantics=("arbitrary", "arbitrary", "arbitrary"),
            vmem_limit_bytes=VMEM_LIMIT_BYTES),
        name="attend_prompt",
    )(*lams, qt, k, vt)


def _attend_sample_kernel(lq1_ref, lk1_ref, lq2_ref, lk2_ref, q_ref, kn_ref, vn_ref, kc_ref, vc_ref,
                          o_ref, *, lambda_init):
    lam = _lambda_value(lq1_ref, lk1_ref, lq2_ref, lk2_ref, lambda_init)
    lane = _lane_iota((q_ref.shape[0], LANES))
    for h in range(DIFF_HEADS):
        sl = slice(h * DIFF_DV, (h + 1) * DIFF_DV)
        q, kn, vn = q_ref[:, sl], kn_ref[:, sl], vn_ref[:, sl]
        past = kc_ref.shape[0] // DIFF_HEADS
        kc = kc_ref[pl.ds(h, past, stride=DIFF_HEADS), :].astype(BF16)
        vc = vc_ref[pl.ds(h, past, stride=DIFF_HEADS), :].astype(BF16)
        zero = jnp.zeros_like(q)
        w_c = None
        w_n = None
        for c in range(2):
            qc = jnp.where(lane < DIFF_DK, q, zero) if c == 0 else jnp.where(lane < DIFF_DK, zero, q)
            s_c = _dot_nt(qc, kc)
            s_n = _dot_nt(qc, kn)
            m = jnp.maximum(jnp.max(s_c, axis=-1, keepdims=True), jnp.max(s_n, axis=-1, keepdims=True))
            p_c = jnp.exp2(s_c - m)
            p_n = jnp.exp2(s_n - m)
            inv = 1.0 / (jnp.sum(p_c, axis=-1, keepdims=True) + jnp.sum(p_n, axis=-1, keepdims=True))
            if c == 0:
                w_c, w_n = p_c * inv, p_n * inv
            else:
                w_c, w_n = w_c - lam * (p_c * inv), w_n - lam * (p_n * inv)
        o_ref[:, sl] = _dot(w_c.astype(BF16), vc) + _dot(w_n.astype(BF16), vn)


def _attend_sample(q, kn, vn, cache_k, cache_v, lams, lambda_init, batch, seq):
    rows = cache_k.shape[0] // batch
    lam_spec = pl.BlockSpec((1, DIFF_DK), lambda b: (0, 0))
    tok = pl.BlockSpec((seq, DIFF_WIDTH), lambda b: (b, 0))
    cache = pl.BlockSpec((rows, DIFF_DV), lambda b: (b, 0))
    return pl.pallas_call(
        functools.partial(_attend_sample_kernel, lambda_init=lambda_init),
        grid=(batch,),
        in_specs=[lam_spec] * 4 + [tok, tok, tok, cache, cache],
        out_specs=tok,
        out_shape=jax.ShapeDtypeStruct((batch * seq, DIFF_WIDTH), F32),
        compiler_params=pltpu.CompilerParams(
            dimension_semantics=("arbitrary",), vmem_limit_bytes=VMEM_LIMIT_BYTES),
        name="attend_sample",
    )(*lams, q, kn, vn, cache_k, cache_v)


def _silu(g):
    return g * (1.0 / (1.0 + jnp.exp(-g)))


def _merge_kernel(ret_ref, rg_ref, dif_ref, dg_ref, x_ref, w_ref, rng_ref, dng_ref, fg_ref, y_ref,
                  *, diff_scale):
    tile = x_ref.shape[0]
    lane = _lane_iota((tile, LANES))
    low = lane < RET_DV
    ret_parts, dif_parts = [], []
    for c in range(RET_WIDTH // LANES):
        sl = slice(c * LANES, (c + 1) * LANES)
        r = ret_ref[:, sl]
        sq = r * r
        ss_a = jnp.sum(jnp.where(low, sq, 0.0), axis=-1, keepdims=True)
        ss_b = jnp.sum(jnp.where(low, 0.0, sq), axis=-1, keepdims=True)
        ms = jnp.where(low, ss_a, ss_b) * (1.0 / RET_DV)
        normed = (r * lax.rsqrt(ms + EPS)) * rng_ref[:, sl]
        ret_parts.append((_silu(rg_ref[:, sl]) * normed).astype(BF16))
    for c in range(DIFF_HEADS):
        sl = slice(c * DIFF_DV, (c + 1) * DIFF_DV)
        d = dif_ref[:, sl]
        ms = jnp.mean(d * d, axis=-1, keepdims=True)
        normed = ((d * lax.rsqrt(ms + EPS)) * dng_ref[:, sl]) * diff_scale
        dif_parts.append((_silu(dg_ref[:, sl]) * normed).astype(BF16))
    mixed_ret = jnp.concatenate(ret_parts, axis=-1)
    mixed_dif = jnp.concatenate(dif_parts, axis=-1)
    out = _dot(mixed_ret, w_ref[:RET_WIDTH, :]) + _dot(mixed_dif, w_ref[RET_WIDTH:, :])
    xo = x_ref[...] + out
    ms = jnp.mean(xo * xo, axis=-1, keepdims=True)
    y_ref[...] = (xo * lax.rsqrt(ms + EPS)) * fg_ref[...]


def _merge(ret_o, rg, dif_o, dg, x2d, w_out_bf16, ret_norm_g, diff_norm_g, final_g, diff_scale):
    n, d = x2d.shape
    tile = PROJ_TILE
    row = lambda i: (i, 0)
    const = lambda i: (0, 0)
    half = pl.BlockSpec((tile, SEG), row)
    return pl.pallas_call(
        functools.partial(_merge_kernel, diff_scale=diff_scale),
        grid=(n // tile,),
        in_specs=[half, half, half, half,
                  pl.BlockSpec((tile, d), row),
                  pl.BlockSpec((RET_WIDTH + DIFF_WIDTH, d), const, pipeline_mode=pl.Buffered(1)),
                  pl.BlockSpec((1, RET_WIDTH), const),
                  pl.BlockSpec((1, DIFF_WIDTH), const),
                  pl.BlockSpec((1, d), const)],
        out_specs=pl.BlockSpec((tile, d), row),
        out_shape=jax.ShapeDtypeStruct((n, d), F32),
        compiler_params=pltpu.CompilerParams(
            dimension_semantics=("arbitrary",), vmem_limit_bytes=VMEM_LIMIT_BYTES),
        name="merge",
    )(ret_o, rg, dif_o, dg, x2d, w_out_bf16, ret_norm_g, diff_norm_g, final_g)


def _rope_tables(pos):
    def tables(dim, theta, block):
        inv_freq = 1.0 / (theta ** (jnp.arange(0, dim, 2, dtype=F32) / dim))
        ang = pos[:, None] * inv_freq[None, :]
        cos, sin = jnp.cos(ang), jnp.sin(ang)
        pad = block - dim
        ones = jnp.ones((pos.shape[0], pad), F32)
        zeros = jnp.zeros((pos.shape[0], pad), F32)
        cos_b = jnp.concatenate([cos, cos, ones], axis=-1)
        sin_b = jnp.concatenate([-sin, sin, zeros], axis=-1)
        reps = LANES // block
        return jnp.tile(cos_b, (1, reps)), jnp.tile(sin_b, (1, reps))

    rcos, rsin = tables(RET_DK, RET_THETA, RET_DK)
    dcos, dsin = tables(ROPE_DIM, ROPE_THETA, DIFF_DK)
    return rcos, rsin, dcos, dsin


def kernel(x_prompt, x_sample, cache_k, cache_v, state_ret, norm_g, w_in, w_out, ret_norm_g, diff_norm_g,
           lam_q1, lam_k1, lam_q2, lam_k2, final_norm_g):
    depth = w_in.shape[0]
    assert depth == 1, "single-layer trunk"
    bp, lp, d = x_prompt.shape
    bs, ls, _ = x_sample.shape
    past = cache_k.shape[2]
    lambda_init = 0.8 - 0.6 * math.exp(-0.3 * 0)
    log_g = jnp.log1p(-jnp.exp2(-5.0 - jnp.arange(RET_HEADS, dtype=F32)))

    w_in_b = w_in[0].astype(BF16)
    w_out_b = w_out[0].astype(BF16)
    g_in = norm_g[0][None, :]
    rng = ret_norm_g[0][None, :]
    dng = jnp.tile(diff_norm_g[0], DIFF_HEADS)[None, :]
    fg = final_norm_g[None, :]
    lams = (lam_q1, lam_k1, lam_q2, lam_k2)

    def run_group(x, pos, s0, seq_tiles, reps, ret_chunk, transposed, attend):
        b, l, _ = x.shape
        x2d = x.reshape(b * l, d)
        tabs = _rope_tables(pos)
        if reps[1] > 1:
            tabs = tuple(jnp.tile(t, (reps[1], 1)) for t in tabs)
        rq, rk, rv, rg, dq, dkb, dvb, dg, dk, dv = _project(
            x2d, g_in, w_in_b, tabs, seq_tiles, reps[0], transposed)
        ret_o, s_bd = _retain(rq, rk, rv, _state_to_blockdiag(s0), log_g, b, l, ret_chunk)
        dif_o = attend(dq, dkb, dvb)
        y = _merge(ret_o, rg, dif_o, dg, x2d, w_out_b, rng, dng, fg, 1.0 - lambda_init)
        return (y.reshape(b, l, d), _blockdiag_to_state(s_bd)[None],
                dk.reshape(1, b, l, DIFF_HEADS, DIFF_DV), dv.reshape(1, b, l, DIFF_HEADS, DIFF_DV))

    pos_p = jnp.arange(lp, dtype=F32)
    zero_state = jnp.zeros((bp, RET_HEADS, RET_DK, RET_DV), F32)
    y_p, s_p, k_p, v_p = run_group(
        x_prompt, pos_p, zero_state, lp // PROJ_TILE, (bp, 1), RET_CHUNK_PROMPT, True,
        lambda qt, k, vt: _attend_prompt(qt, k, vt, lams, lambda_init, bp, lp))

    pos_s = past + jnp.arange(ls, dtype=F32)
    per_tile = PROJ_TILE // ls
    ck = cache_k[0].reshape(bs * past * DIFF_HEADS, DIFF_DV)
    cv = cache_v[0].reshape(bs * past * DIFF_HEADS, DIFF_DV)
    y_s, s_s, k_s, v_s = run_group(
        x_sample, pos_s, state_ret[0], 1, (bs // per_tile, per_tile), ls, False,
        lambda q, k, v: _attend_sample(q, k, v, ck, cv, lams, lambda_init, bs, ls))

    return (y_p, y_s, s_p, s_s, k_p, v_p, k_s, v_s)
```
